```python
import math
import jax, jax.numpy as jnp
from jax import lax
import numpy as np

D_MODEL = 1024
BATCH = 1
SEQ = 16384
DEPTH = 2

N_A = DEPTH // 2
N_B = DEPTH - N_A
CONV_WIDTH = 31
D_FF = ((8 * D_MODEL + 3 * 256 - 1) // (3 * 256)) * 256
HEAD_DIM = 64
N_Q_HEADS = D_MODEL // HEAD_DIM
N_KV_HEADS = 2
GROUP = N_Q_HEADS // N_KV_HEADS
WINDOW = 128
BLOCK = 128
ALIBI_MAX = 8.0
ALPHA = (2.0 * DEPTH) ** 0.25
BETA = (8.0 * DEPTH) ** -0.25
LN_EPS = 1e-5
NEG_INF = -1e30

kernel_name = "yoco_conformer_swa_sink_alibi_deepnorm"


def layer_norm(x, g, b):
    xf = x.astype(jnp.float32)
    mu = jnp.mean(xf, axis=-1, keepdims=True)
    var = jnp.mean(jnp.square(xf - mu), axis=-1, keepdims=True)
    y = (xf - mu) * lax.rsqrt(var + LN_EPS)
    return (y * g.astype(jnp.float32) + b.astype(jnp.float32)).astype(x.dtype)


def conformer_conv(x, w_pw1, b_pw1, w_dw, b_dw, ln_g, ln_b, w_pw2, b_pw2):
    h = x @ w_pw1 + b_pw1
    h = h[..., :D_MODEL] * jax.nn.sigmoid(h[..., D_MODEL:])
    h = lax.conv_general_dilated(
        h, w_dw[:, None, :].astype(h.dtype), window_strides=(1,),
        padding=[(CONV_WIDTH - 1, 0)],
        dimension_numbers=("NWC", "WIO", "NWC"),
        feature_group_count=D_MODEL) + b_dw
    h = jax.nn.silu(layer_norm(h, ln_g, ln_b))
    return h @ w_pw2 + b_pw2


def swiglu(x, w_gate, w_up, w_down):
    return (jax.nn.silu(x @ w_gate) * (x @ w_up)) @ w_down


def banded_blocks(t):
    b, s = t.shape[0], t.shape[1]
    nb = s // BLOCK
    pad = jnp.zeros((b, BLOCK) + t.shape[2:], t.dtype)
    prev = jnp.concatenate([pad, t[:, :s - BLOCK]], axis=1).reshape(b, nb, BLOCK, *t.shape[2:])
    cur = t.reshape(b, nb, BLOCK, *t.shape[2:])
    return jnp.concatenate([prev, cur], axis=2)


def shared_kv(h, w_k, b_k, w_v, b_v):
    b, s, _ = h.shape
    k = (h @ w_k + b_k).reshape(b, s, N_KV_HEADS, HEAD_DIM)
    v = (h @ w_v + b_v).reshape(b, s, N_KV_HEADS, HEAD_DIM)
    return banded_blocks(k), banded_blocks(v)


def window_attention(x, k_blk, v_blk, w_q, b_q, sinks, w_o, b_o):
    b, s, _ = x.shape
    nb = s // BLOCK
    q = (x @ w_q + b_q).reshape(b, nb, BLOCK, N_KV_HEADS, GROUP, HEAD_DIM)
    scores = jnp.einsum("bnikgd,bnjkd->bnkgij", q, k_blk).astype(jnp.float32)
    scores = scores * (1.0 / math.sqrt(HEAD_DIM))
    qi = jnp.arange(BLOCK)[:, None]
    kj = jnp.arange(2 * BLOCK)[None, :]
    delta = qi + BLOCK - kj
    key_pos = jnp.arange(nb)[:, None, None] * BLOCK - BLOCK + kj[None]
    valid = (delta >= 0) & (delta < WINDOW) & (key_pos >= 0)
    slopes = jnp.exp2(-ALIBI_MAX * jnp.arange(1, N_Q_HEADS + 1, dtype=jnp.float32) / N_Q_HEADS)
    slopes = slopes.reshape(N_KV_HEADS, GROUP)
    scores = scores - slopes[None, None, :, :, None, None] * delta.astype(jnp.float32)[None, None, None, None]
    scores = jnp.where(valid[None, :, None, None], scores, NEG_INF)
    sink = jnp.broadcast_to(
        sinks.astype(jnp.float32).reshape(N_KV_HEADS, GROUP)[None, None, :, :, None, None],
        scores.shape[:-1] + (1,))
    probs = jax.nn.softmax(jnp.concatenate([scores, sink], axis=-1), axis=-1)[..., :-1]
    o = jnp.einsum("bnkgij,bnjkd->bnikgd", probs.astype(v_blk.dtype), v_blk)
    o = o.reshape(b, s, N_Q_HEADS * HEAD_DIM)
    return o @ w_o + b_o


def setup_inputs(seed: int = 0) -> dict:
    key = jax.random.key(seed)
    ks = jax.random.split(key, 40)
    f32 = jnp.float32
    D, F, HD = D_MODEL, D_FF, N_Q_HEADS * HEAD_DIM
    KVD = N_KV_HEADS * HEAD_DIM

    def nrm(k, shape, scale):
        return jax.random.normal(k, shape, f32) * scale

    def gain(k, shape):
        return 1.0 + 0.05 * jax.random.normal(k, shape, f32)

    return {
        "x": nrm(ks[0], (BATCH, SEQ, D), 1.0),
        "conv_w_pw1": nrm(ks[1], (N_A, D, 2 * D), D ** -0.5),
        "conv_b_pw1": nrm(ks[2], (N_A, 2 * D), 0.02),
        "conv_w_dw": nrm(ks[3], (N_A, CONV_WIDTH, D), CONV_WIDTH ** -0.5),
        "conv_b_dw": nrm(ks[4], (N_A, D), 0.02),
        "conv_ln_g": gain(ks[5], (N_A, D)),
        "conv_ln_b": nrm(ks[6], (N_A, D), 0.02),
        "conv_w_pw2": nrm(ks[7], (N_A, D, D), BETA * D ** -0.5),
        "conv_b_pw2": nrm(ks[8], (N_A, D), 0.02),
        "kv_w_k": nrm(ks[9], (D, KVD), D ** -0.5),
        "kv_b_k": nrm(ks[10], (KVD,), 0.02),
        "kv_w_v": nrm(ks[11], (D, KVD), BETA * D ** -0.5),
        "kv_b_v": nrm(ks[12], (KVD,), 0.02),
        "attn_w_q": nrm(ks[13], (N_B, D, HD), D ** -0.5),
        "attn_b_q": nrm(ks[14], (N_B, HD), 0.02),
        "attn_sinks": nrm(ks[15], (N_B, N_Q_HEADS), 1.0),
        "attn_w_o": nrm(ks[16], (N_B, HD, D), BETA * HD ** -0.5),
        "attn_b_o": nrm(ks[17], (N_B, D), 0.02),
        "ffn_w_gate": nrm(ks[18], (DEPTH, D, F), D ** -0.5),
        "ffn_w_up": nrm(ks[19], (DEPTH, D, F), D ** -0.5),
        "ffn_w_down": nrm(ks[20], (DEPTH, F, D), BETA * F ** -0.5),
        "ln_mix_g": gain(ks[21], (DEPTH, D)),
        "ln_mix_b": nrm(ks[22], (DEPTH, D), 0.02),
        "ln_ffn_g": gain(ks[23], (DEPTH, D)),
        "ln_ffn_b": nrm(ks[24], (DEPTH, D), 0.02),
    }


def reference(x, conv_w_pw1, conv_b_pw1, conv_w_dw, conv_b_dw, conv_ln_g, conv_ln_b,
              conv_w_pw2, conv_b_pw2, kv_w_k, kv_b_k, kv_w_v, kv_b_v,
              attn_w_q, attn_b_q, attn_sinks, attn_w_o, attn_b_o,
              ffn_w_gate, ffn_w_up, ffn_w_down,
              ln_mix_g, ln_mix_b, ln_ffn_g, ln_ffn_b):
    k_blk, v_blk = None, None
    for layer in range(DEPTH):
        if layer < N_A:
            a = layer
            m = conformer_conv(x, conv_w_pw1[a], conv_b_pw1[a], conv_w_dw[a], conv_b_dw[a],
                               conv_ln_g[a], conv_ln_b[a], conv_w_pw2[a], conv_b_pw2[a])
        else:
            l = layer - N_A
            m = window_attention(x, k_blk, v_blk, attn_w_q[l], attn_b_q[l], attn_sinks[l],
                                 attn_w_o[l], attn_b_o[l])
        x = layer_norm(ALPHA * x + m, ln_mix_g[layer], ln_mix_b[layer])
        f = swiglu(x, ffn_w_gate[layer], ffn_w_up[layer], ffn_w_down[layer])
        x = layer_norm(ALPHA * x + f, ln_ffn_g[layer], ln_ffn_b[layer])
        if layer == N_A - 1:
            k_blk, v_blk = shared_kv(x, kv_w_k, kv_b_k, kv_w_v, kv_b_v)
    return x
```

```python
import functools
import math

import jax
import jax.numpy as jnp
from jax import lax
from jax.experimental import pallas as pl
from jax.experimental.pallas import tpu as pltpu

CONV_WIDTH = 31
HEAD_DIM = 64
N_KV_HEADS = 2
WINDOW = 128
BLOCK = 128
ALIBI_MAX = 8.0
DEPTH = 2
ALPHA = (2.0 * DEPTH) ** 0.25
LN_EPS = 1e-5
NEG_INF = -1e30

LANES = 128
SUBLANES = 8
MXU_DIM = 256
VMEM_LIMIT_BYTES = 56 * 1024 * 1024

ROW_TILE = 512
CONV_HALO = 32
CONV_CHUNK = 16
FFN_CHUNK = MXU_DIM

_BF16 = jnp.bfloat16
_F32 = jnp.float32


def _dot(a, b):
    return jnp.dot(a, b, preferred_element_type=_F32)


def _layer_norm(y, g, b):
    mu = jnp.mean(y, axis=-1, keepdims=True)
    d = y - mu
    var = jnp.mean(d * d, axis=-1, keepdims=True)
    return d * lax.rsqrt(var + LN_EPS) * g + b


def _sigmoid(x):
    return 1.0 / (1.0 + jnp.exp(-x))


def _const_spec(shape):
    zeros = (0,) * len(shape)
    return pl.BlockSpec(shape, lambda i: zeros)


def _row_spec(rows, cols):
    return pl.BlockSpec((rows, cols), lambda i: (i, 0))


def _params():
    return pltpu.CompilerParams(
        dimension_semantics=("arbitrary",), vmem_limit_bytes=VMEM_LIMIT_BYTES)


def _conv_mixer_kernel(x_ref, w1_ref, b1_ref, wdw_ref, bdw_ref, cg_ref, cb_ref,
                       w2_ref, b2_ref, mg_ref, mb_ref, o_ref, h_ref, c_ref):
    tm, d = x_ref.shape

    @pl.when(pl.program_id(0) == 0)
    def _():
        h_ref[0:CONV_HALO, :] = jnp.zeros((CONV_HALO, d), _F32)

    xb = x_ref[...].astype(_BF16)
    a = _dot(xb, w1_ref[:, 0:d]) + b1_ref[:, 0:d]
    g = _dot(xb, w1_ref[:, d:2 * d]) + b1_ref[:, d:2 * d]
    h_ref[CONV_HALO:CONV_HALO + tm, :] = a * _sigmoid(g)

    first = CONV_HALO - (CONV_WIDTH - 1)

    def conv_chunk(c, carry):
        r0 = pl.multiple_of(c * CONV_CHUNK, CONV_CHUNK)
        win = h_ref[pl.ds(r0, CONV_CHUNK + CONV_HALO), :]
        acc = None
        for shift in range(SUBLANES):
            shifted = win if shift == 0 else win[shift:shift + CONV_CHUNK + CONV_HALO - SUBLANES, :]
            for base in range(0, CONV_HALO + 1, SUBLANES):
                k = base + shift - first
                if not 0 <= k < CONV_WIDTH:
                    continue
                term = shifted[base:base + CONV_CHUNK, :] * wdw_ref[k:k + 1, :]
                acc = term if acc is None else acc + term
        acc = acc + bdw_ref[...]
        y = _layer_norm(acc, cg_ref[...], cb_ref[...])
        c_ref[pl.ds(r0, CONV_CHUNK), :] = (y * _sigmoid(y)).astype(_BF16)
        return carry

    lax.fori_loop(0, tm // CONV_CHUNK, conv_chunk, 0)
    h_ref[0:CONV_HALO, :] = h_ref[tm:tm + CONV_HALO, :]

    m = _dot(c_ref[...], w2_ref[...]) + b2_ref[...]
    o_ref[...] = _layer_norm(ALPHA * x_ref[...] + m, mg_ref[...], mb_ref[...])


def _conv_mixer(x, w1, b1, wdw, bdw, cg, cb, w2, b2, mg, mb):
    s, d = x.shape
    tm = ROW_TILE
    return pl.pallas_call(
        _conv_mixer_kernel,
        grid=(s // tm,),
        in_specs=[
            _row_spec(tm, d),
            _const_spec((d, 2 * d)), _const_spec((1, 2 * d)),
            _const_spec((CONV_WIDTH, d)), _const_spec((1, d)),
            _const_spec((1, d)), _const_spec((1, d)),
            _const_spec((d, d)), _const_spec((1, d)),
            _const_spec((1, d)), _const_spec((1, d)),
        ],
        out_specs=_row_spec(tm, d),
        out_shape=jax.ShapeDtypeStruct((s, d), _F32),
        scratch_shapes=[
            pltpu.VMEM((CONV_HALO + tm, d), _F32),
            pltpu.VMEM((tm, d), _BF16),
        ],
        compiler_params=_params(),
        name="conv_mixer",
    )(x, w1, b1, wdw, bdw, cg, cb, w2, b2, mg, mb)


def _swiglu_ln(x, wg_ref, wu_ref, wd_ref, g_ref, b_ref):
    f = wg_ref.shape[1]
    xb = x.astype(_BF16)
    acc = None
    for c0 in range(0, f, FFN_CHUNK):
        gate = _dot(xb, wg_ref[:, c0:c0 + FFN_CHUNK])
        up = _dot(xb, wu_ref[:, c0:c0 + FFN_CHUNK])
        act = (gate * _sigmoid(gate) * up).astype(_BF16)
        part = _dot(act, wd_ref[c0:c0 + FFN_CHUNK, :])
        acc = part if acc is None else acc + part
    return _layer_norm(ALPHA * x + acc, g_ref[...], b_ref[...])


def _ffn_kernel(x_ref, wg_ref, wu_ref, wd_ref, g_ref, b_ref, o_ref):
    o_ref[...] = _swiglu_ln(x_ref[...], wg_ref, wu_ref, wd_ref, g_ref, b_ref)


def _ffn_kv_kernel(x_ref, wg_ref, wu_ref, wd_ref, g_ref, b_ref, wkv_ref, bkv_ref,
                   o_ref, kv_ref):
    y = _swiglu_ln(x_ref[...], wg_ref, wu_ref, wd_ref, g_ref, b_ref)
    o_ref[...] = y
    kv = _dot(y.astype(_BF16), wkv_ref[...]) + bkv_ref[...]
    swapped = jnp.concatenate(
        [pltpu.roll(kv[:, c0:c0 + LANES], HEAD_DIM, axis=1)
         for c0 in range(0, kv.shape[1], LANES)], axis=1)
    low = lax.broadcasted_iota(jnp.int32, kv.shape, 1) % LANES < HEAD_DIM
    zero = jnp.zeros_like(kv)
    variants = (
        jnp.where(low, kv, zero), jnp.where(low, zero, swapped),
        jnp.where(low, swapped, zero), jnp.where(low, zero, kv),
    )
    for idx, var in enumerate(variants):
        kv_ref[idx] = var.astype(_BF16)


def _ffn(x, wg, wu, wd, g, b, kv_proj=None):
    s, d = x.shape
    f = wg.shape[1]
    tm = ROW_TILE
    in_specs = [
        _row_spec(tm, d),
        _const_spec((d, f)), _const_spec((d, f)), _const_spec((f, d)),
        _const_spec((1, d)), _const_spec((1, d)),
    ]
    if kv_proj is None:
        return pl.pallas_call(
            _ffn_kernel, grid=(s // tm,), in_specs=in_specs,
            out_specs=_row_spec(tm, d),
            out_shape=jax.ShapeDtypeStruct((s, d), _F32),
            compiler_params=_params(), name="ffn",
        )(x, wg, wu, wd, g, b)
    wkv, bkv = kv_proj
    kvd = wkv.shape[1]
    return pl.pallas_call(
        _ffn_kv_kernel, grid=(s // tm,),
        in_specs=in_specs + [_const_spec((d, kvd)), _const_spec((1, kvd))],
        out_specs=[_row_spec(tm, d), pl.BlockSpec((4, tm, kvd), lambda i: (0, i, 0))],
        out_shape=[jax.ShapeDtypeStruct((s, d), _F32),
                   jax.ShapeDtypeStruct((4, s, kvd), _BF16)],
        compiler_params=_params(), name="ffn_kv",
    )(x, wg, wu, wd, g, b, wkv, bkv)


_PAIRS = 4
_QROWS = _PAIRS * BLOCK
_KEYS = 2 * BLOCK


def _attn_kernel(x_ref, kv_ref, wq_ref, bq_ref, wo_ref, bo_ref, sink_ref,
                 mg_ref, mb_ref, o_ref, kvbuf_ref, q_ref, ob_ref, bias_ref):
    tq, d = x_ref.shape
    step = pl.program_id(0)
    n_groups = N_KV_HEADS
    n_heads = d // HEAD_DIM
    group = n_heads // n_groups

    @pl.when(step == 0)
    def _():
        kvbuf_ref[:, 0:BLOCK, :] = jnp.zeros((4, BLOCK, kvbuf_ref.shape[2]), _BF16)
        qi = lax.broadcasted_iota(jnp.int32, (BLOCK, _KEYS), 0)
        kj = lax.broadcasted_iota(jnp.int32, (BLOCK, _KEYS), 1)
        delta = qi + BLOCK - kj
        valid = (delta >= 0) & (delta < WINDOW)
        valid_first = valid & (kj >= BLOCK)
        dist = delta.astype(_F32)
        for g in range(n_groups):
            for p in range(_PAIRS):
                for e in range(2):
                    head = g * group + 2 * p + e
                    slope = 2.0 ** (-ALIBI_MAX * (head + 1) / n_heads)
                    pen = -slope * dist
                    rows = slice(p * BLOCK, (p + 1) * BLOCK)
                    cols = slice(e * _KEYS, (e + 1) * _KEYS)
                    bias_ref[0, g, rows, cols] = jnp.where(valid, pen, NEG_INF)
                    bias_ref[1, g, rows, cols] = jnp.where(valid_first, pen, NEG_INF)

    kvbuf_ref[:, BLOCK:BLOCK + tq, :] = kv_ref[...]
    x = x_ref[...]
    q = (_dot(x.astype(_BF16), wq_ref[...]) + bq_ref[...]) * (1.0 / math.sqrt(HEAD_DIM))
    q_ref[...] = q.astype(_BF16)

    lane = lax.broadcasted_iota(jnp.int32, (_QROWS, LANES), 1)
    for blk in range(tq // BLOCK):
        r0 = blk * BLOCK
        variant = (step == 0).astype(jnp.int32) if blk == 0 else 0
        for g in range(n_groups):
            qs = jnp.concatenate(
                [q_ref[r0:r0 + BLOCK, (g * _PAIRS + p) * LANES:(g * _PAIRS + p + 1) * LANES]
                 for p in range(_PAIRS)], axis=0)
            kcols = slice(0, LANES)
            vcols = slice(LANES, 2 * LANES)
            kcat = jnp.concatenate(
                [kvbuf_ref[2 * g, r0:r0 + _KEYS, kcols],
                 kvbuf_ref[2 * g + 1, r0:r0 + _KEYS, kcols]], axis=0)
            vcat = jnp.concatenate(
                [kvbuf_ref[2 * g, r0:r0 + _KEYS, vcols],
                 kvbuf_ref[2 * g + 1, r0:r0 + _KEYS, vcols]], axis=0)
            s = lax.dot_general(qs, kcat, (((1,), (1,)), ((), ())),
                                preferred_element_type=_F32)
            s = s + bias_ref[variant, g]
            probs, inv = [], []
            for e in range(2):
                se = s[:, e * _KEYS:(e + 1) * _KEYS]
                sink = sink_ref[g, e]
                m = jnp.maximum(jnp.max(se, axis=-1, keepdims=True), sink)
                pe = jnp.exp(se - m)
                denom = jnp.sum(pe, axis=-1, keepdims=True) + jnp.exp(sink - m)
                probs.append(pe.astype(_BF16))
                inv.append(1.0 / denom)
            o = _dot(jnp.concatenate(probs, axis=1), vcat)
            o = o * jnp.where(lane < HEAD_DIM, inv[0], inv[1])
            for p in range(_PAIRS):
                c0 = (g * _PAIRS + p) * LANES
                ob_ref[r0:r0 + BLOCK, c0:c0 + LANES] = o[p * BLOCK:(p + 1) * BLOCK, :].astype(_BF16)

    kvbuf_ref[:, 0:BLOCK, :] = kvbuf_ref[:, tq:tq + BLOCK, :]
    attn = _dot(ob_ref[...], wo_ref[...]) + bo_ref[...]
    o_ref[...] = _layer_norm(ALPHA * x + attn, mg_ref[...], mb_ref[...])


def _attention(x, kv, wq, bq, wo, bo, sink_cols, mg, mb):
    s, d = x.shape
    tq = ROW_TILE
    kvd = kv.shape[2]
    return pl.pallas_call(
        _attn_kernel,
        grid=(s // tq,),
        in_specs=[
            _row_spec(tq, d),
            pl.BlockSpec((4, tq, kvd), lambda i: (0, i, 0)),
            _const_spec((d, d)), _const_spec((1, d)),
            _const_spec((d, d)), _const_spec((1, d)),
            _const_spec(sink_cols.shape),
            _const_spec((1, d)), _const_spec((1, d)),
        ],
        out_specs=_row_spec(tq, d),
        out_shape=jax.ShapeDtypeStruct((s, d), _F32),
        scratch_shapes=[
            pltpu.VMEM((4, BLOCK + tq, kvd), _BF16),
            pltpu.VMEM((tq, d), _BF16),
            pltpu.VMEM((tq, d), _BF16),
            pltpu.VMEM((2, N_KV_HEADS, _QROWS, 2 * _KEYS), _F32),
        ],
        compiler_params=_params(),
        name="swa_attention",
    )(x, kv, wq, bq, wo, bo, sink_cols, mg, mb)


def kernel(x, conv_w_pw1, conv_b_pw1, conv_w_dw, conv_b_dw, conv_ln_g, conv_ln_b, conv_w_pw2, conv_b_pw2, kv_w_k, kv_b_k, kv_w_v, kv_b_v, attn_w_q, attn_b_q, attn_sinks, attn_w_o, attn_b_o, ffn_w_gate, ffn_w_up, ffn_w_down, ln_mix_g, ln_mix_b, ln_ffn_g, ln_ffn_b):
    batch, seq, d = x.shape
    assert batch == 1, "tiles carry conv/attention history along the sequence"
    assert seq % ROW_TILE == 0 and d % (2 * HEAD_DIM) == 0
    assert conv_w_pw1.shape[0] == 1 and attn_w_q.shape[0] == 1
    assert d // HEAD_DIM == 2 * _PAIRS * N_KV_HEADS

    def row(v):
        return v.reshape(1, -1).astype(_F32)

    h = x.reshape(seq, d)
    h = _conv_mixer(
        h, conv_w_pw1[0].astype(_BF16), row(conv_b_pw1[0]), conv_w_dw[0], row(conv_b_dw[0]),
        row(conv_ln_g[0]), row(conv_ln_b[0]), conv_w_pw2[0].astype(_BF16), row(conv_b_pw2[0]),
        row(ln_mix_g[0]), row(ln_mix_b[0]))

    wkv = jnp.concatenate([kv_w_k, kv_w_v], axis=1).astype(_BF16)
    bkv = row(jnp.concatenate([kv_b_k, kv_b_v]))
    h, kv = _ffn(
        h, ffn_w_gate[0].astype(_BF16), ffn_w_up[0].astype(_BF16), ffn_w_down[0].astype(_BF16),
        row(ln_ffn_g[0]), row(ln_ffn_b[0]), kv_proj=(wkv, bkv))

    sink_cols = attn_sinks[0].astype(_F32).reshape(N_KV_HEADS, _PAIRS, 2)
    sink_cols = jnp.repeat(sink_cols.transpose(0, 2, 1), BLOCK, axis=2)[..., None]
    h = _attention(
        h, kv, attn_w_q[0].astype(_BF16), row(attn_b_q[0]), attn_w_o[0].astype(_BF16),
        row(attn_b_o[0]), sink_cols, row(ln_mix_g[1]), row(ln_mix_b[1]))

    h = _ffn(
        h, ffn_w_gate[1].astype(_BF16), ffn_w_up[1].astype(_BF16), ffn_w_down[1].astype(_BF16),
        row(ln_ffn_g[1]), row(ln_ffn_b[1]))
    return h.reshape(batch, seq, d)
```

```python
import math

import jax
import jax.numpy as jnp
from jax import lax
from jax.experimental import pallas as pl
from jax.experimental.pallas import tpu as pltpu

CONV_WIDTH = 31
HEAD_DIM = 64
N_KV_HEADS = 2
WINDOW = 128
BLOCK = 128
ALIBI_MAX = 8.0
DEPTH = 2
ALPHA = (2.0 * DEPTH) ** 0.25
LN_EPS = 1e-5
NEG_INF = -1e30

LANES = 128
SUBLANES = 8
MXU_DIM = 256
VMEM_LIMIT_BYTES = 56 * 1024 * 1024

ROW_TILE = 512
CONV_PHASES = 16
CONV_POS_CHUNK = 8
FFN_CHUNK = MXU_DIM

_BF16 = jnp.bfloat16
_F32 = jnp.float32


def _dot(a, b):
    return jnp.dot(a, b, preferred_element_type=_F32)


def _layer_norm(y, g, b):
    mu = jnp.mean(y, axis=-1, keepdims=True)
    d = y - mu
    var = jnp.mean(d * d, axis=-1, keepdims=True)
    return d * lax.rsqrt(var + LN_EPS) * g + b


def _sigmoid(x):
    return 1.0 / (1.0 + jnp.exp(-x))


def _const_spec(shape):
    zeros = (0,) * len(shape)
    return pl.BlockSpec(shape, lambda i: zeros)


def _row_spec(rows, cols):
    return pl.BlockSpec((rows, cols), lambda i: (i, 0))


def _params():
    return pltpu.CompilerParams(
        dimension_semantics=("arbitrary",), vmem_limit_bytes=VMEM_LIMIT_BYTES)


def _conv_mixer_kernel(x_ref, w1_ref, b1_ref, wdw_ref, bdw_ref, cg_ref, cb_ref,
                       w2_ref, b2_ref, mg_ref, mb_ref, o_ref,
                       e_ref, r_ref, wr_ref, cv_ref):
    _, phases, npos, d = x_ref.shape
    tm = phases * npos
    hist = e_ref.shape[0] - npos
    taps = CONV_WIDTH

    @pl.when(pl.program_id(0) == 0)
    def _():
        r_ref[...] = jnp.zeros(r_ref.shape, _F32)
        for k in range(taps):
            wk = jnp.broadcast_to(wdw_ref[k:k + 1, :], (phases, d)).astype(_BF16)
            for lg in range(d // LANES):
                wr_ref[k, lg] = wk[:, lg * LANES:(lg + 1) * LANES]

    xp = pltpu.einshape("sjd->jsd", x_ref[0]).reshape(tm, d)
    xb = xp.astype(_BF16)
    a = _dot(xb, w1_ref[:, 0:d]) + b1_ref[:, 0:d]
    g = _dot(xb, w1_ref[:, d:2 * d]) + b1_ref[:, d:2 * d]
    h = a * _sigmoid(g)

    tail = h[(npos - hist) * phases:, :]
    down = pltpu.roll(tail, 1, axis=0)
    ph = lax.broadcasted_iota(jnp.int32, tail.shape, 0) % phases
    prev = jnp.where(ph == 0, r_ref[...], down)
    r_ref[...] = jnp.concatenate([down[phases:, :], down[0:phases, :]], axis=0)
    prev_b, h_b = prev.astype(_BF16), h.astype(_BF16)
    for lg in range(d // LANES):
        cols = slice(lg * LANES, (lg + 1) * LANES)
        e_ref[0:hist, lg] = prev_b[:, cols].reshape(hist, phases, LANES)
        e_ref[hist:hist + npos, lg] = h_b[:, cols].reshape(npos, phases, LANES)

    @pl.when(pl.program_id(0) >= 0)
    def _():
        for p0 in range(0, npos, CONV_POS_CHUNK):
            base = p0 + hist - (taps - 1)
            for lg in range(d // LANES):
                cols = slice(lg * LANES, (lg + 1) * LANES)
                win = [e_ref[base + q, lg] for q in range(CONV_POS_CHUNK + taps - 1)]
                accs = [None] * CONV_POS_CHUNK
                for k in range(taps):
                    wk = wr_ref[k, lg].astype(_F32)
                    for p in range(CONV_POS_CHUNK):
                        term = win[p + k].astype(_F32) * wk
                        accs[p] = term if accs[p] is None else accs[p] + term
                for p in range(CONV_POS_CHUNK):
                    cv_ref[p0 + p, :, cols] = accs[p]

    conv = cv_ref[...].reshape(tm, d) + bdw_ref[...]
    y = _layer_norm(conv, cg_ref[...], cb_ref[...])
    c = (y * _sigmoid(y)).astype(_BF16)
    m = _dot(c, w2_ref[...]) + b2_ref[...]
    y = _layer_norm(ALPHA * xp + m, mg_ref[...], mb_ref[...])
    o_ref[0] = pltpu.einshape("jsd->sjd", y.reshape(npos, phases, d))


def _conv_mixer(x, w1, b1, wdw, bdw, cg, cb, w2, b2, mg, mb):
    s, d = x.shape
    tm = ROW_TILE
    tiles, npos = s // tm, tm // CONV_PHASES
    hist = npos
    assert CONV_WIDTH - 1 <= hist and npos % CONV_POS_CHUNK == 0
    phase_spec = pl.BlockSpec((1, CONV_PHASES, npos, d), lambda i: (i, 0, 0, 0))
    out = pl.pallas_call(
        _conv_mixer_kernel,
        grid=(tiles,),
        in_specs=[
            phase_spec,
            _const_spec((d, 2 * d)), _const_spec((1, 2 * d)),
            _const_spec((CONV_WIDTH, d)), _const_spec((1, d)),
            _const_spec((1, d)), _const_spec((1, d)),
            _const_spec((d, d)), _const_spec((1, d)),
            _const_spec((1, d)), _const_spec((1, d)),
        ],
        out_specs=phase_spec,
        out_shape=jax.ShapeDtypeStruct((tiles, CONV_PHASES, npos, d), _F32),
        scratch_shapes=[
            pltpu.VMEM((hist + npos, d // LANES, CONV_PHASES, LANES), _BF16),
            pltpu.VMEM((hist * CONV_PHASES, d), _F32),
            pltpu.VMEM((CONV_WIDTH, d // LANES, CONV_PHASES, LANES), _BF16),
            pltpu.VMEM((npos, CONV_PHASES, d), _F32),
        ],
        compiler_params=_params(),
        name="conv_mixer",
    )(x.reshape(tiles, CONV_PHASES, npos, d), w1, b1, wdw, bdw, cg, cb, w2, b2, mg, mb)
    return out.reshape(s, d)


def _swiglu_ln(x, wg_ref, wu_ref, wd_ref, g_ref, b_ref):
    f = wg_ref.shape[1]
    xb = x.astype(_BF16)
    acc = None
    for c0 in range(0, f, FFN_CHUNK):
        gate = _dot(xb, wg_ref[:, c0:c0 + FFN_CHUNK])
        up = _dot(xb, wu_ref[:, c0:c0 + FFN_CHUNK])
        act = (gate * _sigmoid(gate) * up).astype(_BF16)
        part = _dot(act, wd_ref[c0:c0 + FFN_CHUNK, :])
        acc = part if acc is None else acc + part
    return _layer_norm(ALPHA * x + acc, g_ref[...], b_ref[...])


def _ffn_kernel(x_ref, wg_ref, wu_ref, wd_ref, g_ref, b_ref, o_ref):
    o_ref[...] = _swiglu_ln(x_ref[...], wg_ref, wu_ref, wd_ref, g_ref, b_ref)


def _ffn_kv_kernel(x_ref, wg_ref, wu_ref, wd_ref, g_ref, b_ref, wkv_ref, bkv_ref,
                   o_ref, kv_ref):
    y = _swiglu_ln(x_ref[...], wg_ref, wu_ref, wd_ref, g_ref, b_ref)
    o_ref[...] = y
    kv = _dot(y.astype(_BF16), wkv_ref[...]) + bkv_ref[...]
    swapped = jnp.concatenate(
        [pltpu.roll(kv[:, c0:c0 + LANES], HEAD_DIM, axis=1)
         for c0 in range(0, kv.shape[1], LANES)], axis=1)
    low = lax.broadcasted_iota(jnp.int32, kv.shape, 1) % LANES < HEAD_DIM
    zero = jnp.zeros_like(kv)
    variants = (
        jnp.where(low, kv, zero), jnp.where(low, zero, swapped),
        jnp.where(low, swapped, zero), jnp.where(low, zero, kv),
    )
    for idx, var in enumerate(variants):
        kv_ref[idx] = var.astype(_BF16)


def _ffn(x, wg, wu, wd, g, b, kv_proj=None):
    s, d = x.shape
    f = wg.shape[1]
    tm = ROW_TILE
    in_specs = [
        _row_spec(tm, d),
        _const_spec((d, f)), _const_spec((d, f)), _const_spec((f, d)),
        _const_spec((1, d)), _const_spec((1, d)),
    ]
    if kv_proj is None:
        return pl.pallas_call(
            _ffn_kernel, grid=(s // tm,), in_specs=in_specs,
            out_specs=_row_spec(tm, d),
            out_shape=jax.ShapeDtypeStruct((s, d), _F32),
            compiler_params=_params(), name="ffn",
        )(x, wg, wu, wd, g, b)
    wkv, bkv = kv_proj
    kvd = wkv.shape[1]
    return pl.pallas_call(
        _ffn_kv_kernel, grid=(s // tm,),
        in_specs=in_specs + [_const_spec((d, kvd)), _const_spec((1, kvd))],
        out_specs=[_row_spec(tm, d), pl.BlockSpec((4, tm, kvd), lambda i: (0, i, 0))],
        out_shape=[jax.ShapeDtypeStruct((s, d), _F32),
                   jax.ShapeDtypeStruct((4, s, kvd), _BF16)],
        compiler_params=_params(), name="ffn_kv",
    )(x, wg, wu, wd, g, b, wkv, bkv)


_PAIRS = 4
_QROWS = _PAIRS * BLOCK
_KEYS = 2 * BLOCK


def _attn_kernel(x_ref, kv_ref, wq_ref, bq_ref, wo_ref, bo_ref, sink_ref,
                 mg_ref, mb_ref, o_ref, kvbuf_ref, q_ref, ob_ref, bias_ref):
    tq, d = x_ref.shape
    step = pl.program_id(0)
    n_groups = N_KV_HEADS
    n_heads = d // HEAD_DIM
    group = n_heads // n_groups

    @pl.when(step == 0)
    def _():
        kvbuf_ref[:, 0:BLOCK, :] = jnp.zeros((4, BLOCK, kvbuf_ref.shape[2]), _BF16)
        qi = lax.broadcasted_iota(jnp.int32, (BLOCK, _KEYS), 0)
        kj = lax.broadcasted_iota(jnp.int32, (BLOCK, _KEYS), 1)
        delta = qi + BLOCK - kj
        valid = (delta >= 0) & (delta < WINDOW)
        valid_first = valid & (kj >= BLOCK)
        dist = delta.astype(_F32)
        for g in range(n_groups):
            for p in range(_PAIRS):
                for e in range(2):
                    head = g * group + 2 * p + e
                    slope = 2.0 ** (-ALIBI_MAX * (head + 1) / n_heads)
                    pen = -slope * dist
                    rows = slice(p * BLOCK, (p + 1) * BLOCK)
                    cols = slice(e * _KEYS, (e + 1) * _KEYS)
                    bias_ref[0, g, rows, cols] = jnp.where(valid, pen, NEG_INF)
                    bias_ref[1, g, rows, cols] = jnp.where(valid_first, pen, NEG_INF)

    kvbuf_ref[:, BLOCK:BLOCK + tq, :] = kv_ref[...]
    x = x_ref[...]
    q = (_dot(x.astype(_BF16), wq_ref[...]) + bq_ref[...]) * (1.0 / math.sqrt(HEAD_DIM))
    q_ref[...] = q.astype(_BF16)

    lane = lax.broadcasted_iota(jnp.int32, (_QROWS, LANES), 1)
    for blk in range(tq // BLOCK):
        r0 = blk * BLOCK
        variant = (step == 0).astype(jnp.int32) if blk == 0 else 0
        for g in range(n_groups):
            qs = jnp.concatenate(
                [q_ref[r0:r0 + BLOCK, (g * _PAIRS + p) * LANES:(g * _PAIRS + p + 1) * LANES]
                 for p in range(_PAIRS)], axis=0)
            kcols = slice(0, LANES)
            vcols = slice(LANES, 2 * LANES)
            kcat = jnp.concatenate(
                [kvbuf_ref[2 * g, r0:r0 + _KEYS, kcols],
                 kvbuf_ref[2 * g + 1, r0:r0 + _KEYS, kcols]], axis=0)
            vcat = jnp.concatenate(
                [kvbuf_ref[2 * g, r0:r0 + _KEYS, vcols],
                 kvbuf_ref[2 * g + 1, r0:r0 + _KEYS, vcols]], axis=0)
            s = lax.dot_general(qs, kcat, (((1,), (1,)), ((), ())),
                                preferred_element_type=_F32)
            s = s + bias_ref[variant, g]
            probs, inv = [], []
            for e in range(2):
                se = s[:, e * _KEYS:(e + 1) * _KEYS]
                sink = sink_ref[g, e]
                m = jnp.maximum(jnp.max(se, axis=-1, keepdims=True), sink)
                pe = jnp.exp(se - m)
                denom = jnp.sum(pe, axis=-1, keepdims=True) + jnp.exp(sink - m)
                probs.append(pe.astype(_BF16))
                inv.append(1.0 / denom)
            o = _dot(jnp.concatenate(probs, axis=1), vcat)
            o = o * jnp.where(lane < HEAD_DIM, inv[0], inv[1])
            for p in range(_PAIRS):
                c0 = (g * _PAIRS + p) * LANES
                ob_ref[r0:r0 + BLOCK, c0:c0 + LANES] = o[p * BLOCK:(p + 1) * BLOCK, :].astype(_BF16)

    kvbuf_ref[:, 0:BLOCK, :] = kvbuf_ref[:, tq:tq + BLOCK, :]
    attn = _dot(ob_ref[...], wo_ref[...]) + bo_ref[...]
    o_ref[...] = _layer_norm(ALPHA * x + attn, mg_ref[...], mb_ref[...])


def _attention(x, kv, wq, bq, wo, bo, sink_cols, mg, mb):
    s, d = x.shape
    tq = ROW_TILE
    kvd = kv.shape[2]
    return pl.pallas_call(
        _attn_kernel,
        grid=(s // tq,),
        in_specs=[
            _row_spec(tq, d),
            pl.BlockSpec((4, tq, kvd), lambda i: (0, i, 0)),
            _const_spec((d, d)), _const_spec((1, d)),
            _const_spec((d, d)), _const_spec((1, d)),
            _const_spec(sink_cols.shape),
            _const_spec((1, d)), _const_spec((1, d)),
        ],
        out_specs=_row_spec(tq, d),
        out_shape=jax.ShapeDtypeStruct((s, d), _F32),
        scratch_shapes=[
            pltpu.VMEM((4, BLOCK + tq, kvd), _BF16),
            pltpu.VMEM((tq, d), _BF16),
            pltpu.VMEM((tq, d), _BF16),
            pltpu.VMEM((2, N_KV_HEADS, _QROWS, 2 * _KEYS), _F32),
        ],
        compiler_params=_params(),
        name="swa_attention",
    )(x, kv, wq, bq, wo, bo, sink_cols, mg, mb)


def kernel(x, conv_w_pw1, conv_b_pw1, conv_w_dw, conv_b_dw, conv_ln_g, conv_ln_b, conv_w_pw2, conv_b_pw2, kv_w_k, kv_b_k, kv_w_v, kv_b_v, attn_w_q, attn_b_q, attn_sinks, attn_w_o, attn_b_o, ffn_w_gate, ffn_w_up, ffn_w_down, ln_mix_g, ln_mix_b, ln_ffn_g, ln_ffn_b):
    batch, seq, d = x.shape
    assert batch == 1, "tiles carry conv/attention history along the sequence"
    assert seq % ROW_TILE == 0 and d % (2 * HEAD_DIM) == 0
    assert conv_w_pw1.shape[0] == 1 and attn_w_q.shape[0] == 1
    assert d // HEAD_DIM == 2 * _PAIRS * N_KV_HEADS

    def row(v):
        return v.reshape(1, -1).astype(_F32)

    h = x.reshape(seq, d)
    h = _conv_mixer(
        h, conv_w_pw1[0].astype(_BF16), row(conv_b_pw1[0]), conv_w_dw[0], row(conv_b_dw[0]),
        row(conv_ln_g[0]), row(conv_ln_b[0]), conv_w_pw2[0].astype(_BF16), row(conv_b_pw2[0]),
        row(ln_mix_g[0]), row(ln_mix_b[0]))

    wkv = jnp.concatenate([kv_w_k, kv_w_v], axis=1).astype(_BF16)
    bkv = row(jnp.concatenate([kv_b_k, kv_b_v]))
    h, kv = _ffn(
        h, ffn_w_gate[0].astype(_BF16), ffn_w_up[0].astype(_BF16), ffn_w_down[0].astype(_BF16),
        row(ln_ffn_g[0]), row(ln_ffn_b[0]), kv_proj=(wkv, bkv))

    sink_cols = attn_sinks[0].astype(_F32).reshape(N_KV_HEADS, _PAIRS, 2)
    sink_cols = jnp.repeat(sink_cols.transpose(0, 2, 1), BLOCK, axis=2)[..., None]
    h = _attention(
        h, kv, attn_w_q[0].astype(_BF16), row(attn_b_q[0]), attn_w_o[0].astype(_BF16),
        row(attn_b_o[0]), sink_cols, row(ln_mix_g[1]), row(ln_mix_b[1]))

    h = _ffn(
        h, ffn_w_gate[1].astype(_BF16), ffn_w_up[1].astype(_BF16), ffn_w_down[1].astype(_BF16),
        row(ln_ffn_g[1]), row(ln_ffn_b[1]))
    return h.reshape(batch, seq, d)
```

```python
import math

import jax
import jax.numpy as jnp
from jax import lax
from jax.experimental import pallas as pl
from jax.experimental.pallas import tpu as pltpu

CONV_WIDTH = 31
HEAD_DIM = 64
N_KV_HEADS = 2
WINDOW = 128
BLOCK = 128
ALIBI_MAX = 8.0
DEPTH = 2
ALPHA = (2.0 * DEPTH) ** 0.25
LN_EPS = 1e-5
NEG_INF = -1e30

LANES = 128
SUBLANES = 8
MXU_DIM = 256
VMEM_LIMIT_BYTES = 56 * 1024 * 1024

ROW_TILE = 512
CONV_PHASES = 16
CONV_POS_CHUNK = 8
FFN_CHUNK = MXU_DIM

_BF16 = jnp.bfloat16
_F32 = jnp.float32


def _dot(a, b):
    return jnp.dot(a, b, preferred_element_type=_F32)


def _layer_norm(y, g, b):
    mu = jnp.mean(y, axis=-1, keepdims=True)
    d = y - mu
    var = jnp.mean(d * d, axis=-1, keepdims=True)
    return d * lax.rsqrt(var + LN_EPS) * g + b


def _sigmoid(x):
    return 1.0 / (1.0 + jnp.exp(-x))


def _const_spec(shape, index=None):
    index = (0,) * len(shape) if index is None else index
    return pl.BlockSpec(shape, lambda i: index, pipeline_mode=pl.Buffered(1))


def _layer_spec(layer, rows, cols):
    return _const_spec((None, rows, cols), (layer, 0, 0))


def _rows(v):
    return v.reshape(v.shape[0], 1, v.shape[1])


def _row_spec(rows, cols):
    return pl.BlockSpec((rows, cols), lambda i: (i, 0))


def _params():
    return pltpu.CompilerParams(
        dimension_semantics=("arbitrary",), vmem_limit_bytes=VMEM_LIMIT_BYTES)


def _conv_mixer_kernel(x_ref, w1_ref, b1_ref, wdw_ref, bdw_ref, cg_ref, cb_ref,
                       w2_ref, b2_ref, mg_ref, mb_ref, o_ref,
                       e_ref, r_ref, wr_ref, cv_ref):
    _, phases, npos, d = x_ref.shape
    tm = phases * npos
    hist = e_ref.shape[0] - npos
    taps = CONV_WIDTH

    @pl.when(pl.program_id(0) == 0)
    def _():
        r_ref[...] = jnp.zeros(r_ref.shape, _F32)
        for k in range(taps):
            wk = jnp.broadcast_to(wdw_ref[k:k + 1, :], (phases, d)).astype(_BF16)
            for lg in range(d // LANES):
                wr_ref[k, lg] = wk[:, lg * LANES:(lg + 1) * LANES]

    xp = jnp.swapaxes(x_ref[0], 0, 1).reshape(tm, d)
    a = _dot(xp, w1_ref[:, 0:d]) + b1_ref[:, 0:d]
    g = _dot(xp, w1_ref[:, d:2 * d]) + b1_ref[:, d:2 * d]
    h = a * _sigmoid(g)

    tail = h[(npos - hist) * phases:, :]
    down = pltpu.roll(tail, 1, axis=0)
    ph = lax.broadcasted_iota(jnp.int32, tail.shape, 0) % phases
    prev = jnp.where(ph == 0, r_ref[...], down)
    r_ref[...] = jnp.concatenate([down[phases:, :], down[0:phases, :]], axis=0)
    prev_b, h_b = prev.astype(_BF16), h.astype(_BF16)
    for lg in range(d // LANES):
        cols = slice(lg * LANES, (lg + 1) * LANES)
        e_ref[0:hist, lg] = prev_b[:, cols].reshape(hist, phases, LANES)
        e_ref[hist:hist + npos, lg] = h_b[:, cols].reshape(npos, phases, LANES)

    @pl.when(pl.program_id(0) >= 0)
    def _():
        for p0 in range(0, npos, CONV_POS_CHUNK):
            base = p0 + hist - (taps - 1)
            for lg in range(d // LANES):
                cols = slice(lg * LANES, (lg + 1) * LANES)
                win = [e_ref[base + q, lg] for q in range(CONV_POS_CHUNK + taps - 1)]
                accs = [None] * CONV_POS_CHUNK
                for k in range(taps):
                    wk = wr_ref[k, lg].astype(_F32)
                    for p in range(CONV_POS_CHUNK):
                        term = win[p + k].astype(_F32) * wk
                        accs[p] = term if accs[p] is None else accs[p] + term
                for p in range(CONV_POS_CHUNK):
                    cv_ref[p0 + p, :, cols] = accs[p]

    conv = cv_ref[...].reshape(tm, d) + bdw_ref[...]
    y = _layer_norm(conv, cg_ref[...], cb_ref[...])
    m = _dot(y * _sigmoid(y), w2_ref[...]) + b2_ref[...]
    y = _layer_norm(ALPHA * xp + m, mg_ref[...], mb_ref[...])
    o_ref[0] = jnp.swapaxes(y.reshape(npos, phases, d), 0, 1)


def _conv_mixer(x, layer, w1, b1, wdw, bdw, cg, cb, w2, b2, mg, mb):
    s, d = x.shape
    tm = ROW_TILE
    tiles, npos = s // tm, tm // CONV_PHASES
    hist = npos
    assert CONV_WIDTH - 1 <= hist and npos % CONV_POS_CHUNK == 0
    phase_spec = pl.BlockSpec((1, CONV_PHASES, npos, d), lambda i: (i, 0, 0, 0))
    a_layer, trunk_layer = layer
    out = pl.pallas_call(
        _conv_mixer_kernel,
        grid=(tiles,),
        in_specs=[
            phase_spec,
            _layer_spec(a_layer, d, 2 * d), _layer_spec(a_layer, 1, 2 * d),
            _layer_spec(a_layer, CONV_WIDTH, d), _layer_spec(a_layer, 1, d),
            _layer_spec(a_layer, 1, d), _layer_spec(a_layer, 1, d),
            _layer_spec(a_layer, d, d), _layer_spec(a_layer, 1, d),
            _layer_spec(trunk_layer, 1, d), _layer_spec(trunk_layer, 1, d),
        ],
        out_specs=phase_spec,
        out_shape=jax.ShapeDtypeStruct((tiles, CONV_PHASES, npos, d), _F32),
        scratch_shapes=[
            pltpu.VMEM((hist + npos, d // LANES, CONV_PHASES, LANES), _BF16),
            pltpu.VMEM((hist * CONV_PHASES, d), _F32),
            pltpu.VMEM((CONV_WIDTH, d // LANES, CONV_PHASES, LANES), _BF16),
            pltpu.VMEM((npos, CONV_PHASES, d), _F32),
        ],
        compiler_params=_params(),
        name="conv_mixer",
    )(x.reshape(tiles, CONV_PHASES, npos, d), w1, _rows(b1), wdw, _rows(bdw), _rows(cg),
      _rows(cb), w2, _rows(b2), _rows(mg), _rows(mb))
    return out.reshape(s, d)


def _swiglu_ln(x, wg_ref, wu_ref, wd_ref, g_ref, b_ref):
    f = wg_ref.shape[1]
    acc = None
    for c0 in range(0, f, FFN_CHUNK):
        gate = _dot(x, wg_ref[:, c0:c0 + FFN_CHUNK])
        up = _dot(x, wu_ref[:, c0:c0 + FFN_CHUNK])
        part = _dot(gate * _sigmoid(gate) * up, wd_ref[c0:c0 + FFN_CHUNK, :])
        acc = part if acc is None else acc + part
    return _layer_norm(ALPHA * x + acc, g_ref[...], b_ref[...])


def _ffn_kernel(x_ref, wg_ref, wu_ref, wd_ref, g_ref, b_ref, o_ref):
    o_ref[...] = _swiglu_ln(x_ref[...], wg_ref, wu_ref, wd_ref, g_ref, b_ref)


def _ffn_kv_kernel(x_ref, wg_ref, wu_ref, wd_ref, g_ref, b_ref, wkv_ref, bkv_ref,
                   o_ref, kv_ref):
    y = _swiglu_ln(x_ref[...], wg_ref, wu_ref, wd_ref, g_ref, b_ref)
    o_ref[...] = y
    kv = _dot(y, wkv_ref[...]) + bkv_ref[...]
    swapped = jnp.concatenate(
        [pltpu.roll(kv[:, c0:c0 + LANES], HEAD_DIM, axis=1)
         for c0 in range(0, kv.shape[1], LANES)], axis=1)
    low = lax.broadcasted_iota(jnp.int32, kv.shape, 1) % LANES < HEAD_DIM
    zero = jnp.zeros_like(kv)
    variants = (
        jnp.where(low, kv, zero), jnp.where(low, zero, swapped),
        jnp.where(low, swapped, zero), jnp.where(low, zero, kv),
    )
    for idx, var in enumerate(variants):
        kv_ref[idx] = var.astype(_BF16)


def _ffn(x, layer, wg, wu, wd, g, b, kv_proj=None):
    s, d = x.shape
    f = wg.shape[2]
    tm = ROW_TILE
    in_specs = [
        _row_spec(tm, d),
        _layer_spec(layer, d, f), _layer_spec(layer, d, f), _layer_spec(layer, f, d),
        _layer_spec(layer, 1, d), _layer_spec(layer, 1, d),
    ]
    if kv_proj is None:
        return pl.pallas_call(
            _ffn_kernel, grid=(s // tm,), in_specs=in_specs,
            out_specs=_row_spec(tm, d),
            out_shape=jax.ShapeDtypeStruct((s, d), _F32),
            compiler_params=_params(), name="ffn",
        )(x, wg, wu, wd, _rows(g), _rows(b))
    wkv, bkv = kv_proj
    kvd = wkv.shape[1]
    return pl.pallas_call(
        _ffn_kv_kernel, grid=(s // tm,),
        in_specs=in_specs + [_const_spec((d, kvd)), _const_spec((1, kvd))],
        out_specs=[_row_spec(tm, d), pl.BlockSpec((4, tm, kvd), lambda i: (0, i, 0))],
        out_shape=[jax.ShapeDtypeStruct((s, d), _F32),
                   jax.ShapeDtypeStruct((4, s, kvd), _BF16)],
        compiler_params=_params(), name="ffn_kv",
    )(x, wg, wu, wd, _rows(g), _rows(b), wkv, bkv)


_PAIRS = 4
_QROWS = _PAIRS * BLOCK
_KEYS = 2 * BLOCK


def _attn_kernel(x_ref, kv_ref, wq_ref, bq_ref, wo_ref, bo_ref, sink_ref,
                 mg_ref, mb_ref, o_ref, kvbuf_ref, q_ref, ob_ref, bias_ref):
    tq, d = x_ref.shape
    step = pl.program_id(0)
    n_groups = N_KV_HEADS
    n_heads = d // HEAD_DIM
    group = n_heads // n_groups

    @pl.when(step == 0)
    def _():
        kvbuf_ref[:, 0:BLOCK, :] = jnp.zeros((4, BLOCK, kvbuf_ref.shape[2]), _BF16)
        qi = lax.broadcasted_iota(jnp.int32, (BLOCK, _KEYS), 0)
        kj = lax.broadcasted_iota(jnp.int32, (BLOCK, _KEYS), 1)
        delta = qi + BLOCK - kj
        valid = (delta >= 0) & (delta < WINDOW)
        valid_first = valid & (kj >= BLOCK)
        dist = delta.astype(_F32)
        for g in range(n_groups):
            for p in range(_PAIRS):
                for e in range(2):
                    head = g * group + 2 * p + e
                    slope = 2.0 ** (-ALIBI_MAX * (head + 1) / n_heads)
                    pen = -slope * dist
                    rows = slice(p * BLOCK, (p + 1) * BLOCK)
                    cols = slice(e * _KEYS, (e + 1) * _KEYS)
                    bias_ref[0, g, rows, cols] = jnp.where(valid, pen, NEG_INF)
                    bias_ref[1, g, rows, cols] = jnp.where(valid_first, pen, NEG_INF)

    kvbuf_ref[:, BLOCK:BLOCK + tq, :] = kv_ref[...]
    x = x_ref[...]
    q = (_dot(x, wq_ref[...]) + bq_ref[...]) * (1.0 / math.sqrt(HEAD_DIM))
    q_ref[...] = q.astype(_BF16)

    lane = lax.broadcasted_iota(jnp.int32, (_KEYS, LANES), 1)
    ones_lo = jnp.where(lane < HEAD_DIM, 1.0, 0.0).astype(_BF16)
    ones_hi = jnp.where(lane < HEAD_DIM, 0.0, 1.0).astype(_BF16)
    low = lax.broadcasted_iota(jnp.int32, (_QROWS, LANES), 1) < HEAD_DIM
    for blk in range(tq // BLOCK):
        r0 = blk * BLOCK
        variant = (step == 0).astype(jnp.int32) if blk == 0 else 0
        for g in range(n_groups):
            qs = jnp.concatenate(
                [q_ref[r0:r0 + BLOCK, (g * _PAIRS + p) * LANES:(g * _PAIRS + p + 1) * LANES]
                 for p in range(_PAIRS)], axis=0)
            kcols = slice(0, LANES)
            vcols = slice(LANES, 2 * LANES)
            kcat = jnp.concatenate(
                [kvbuf_ref[2 * g, r0:r0 + _KEYS, kcols],
                 kvbuf_ref[2 * g + 1, r0:r0 + _KEYS, kcols]], axis=0)
            vcat = jnp.concatenate(
                [jnp.concatenate([kvbuf_ref[2 * g, r0:r0 + _KEYS, vcols], ones_lo], axis=1),
                 jnp.concatenate([kvbuf_ref[2 * g + 1, r0:r0 + _KEYS, vcols], ones_hi], axis=1)],
                axis=0)
            s = lax.dot_general(qs, kcat, (((1,), (1,)), ((), ())),
                                preferred_element_type=_F32)
            s = s + bias_ref[variant, g]
            probs, sink_terms = [], []
            for e in range(2):
                sink = sink_ref[g, e]
                m = jnp.maximum(jnp.max(s[:, e * _KEYS:(e + 1) * _KEYS], axis=-1, keepdims=True), sink)
                for c0 in range(e * _KEYS, (e + 1) * _KEYS, LANES):
                    probs.append(jnp.exp(s[:, c0:c0 + LANES] - m).astype(_BF16))
                sink_terms.append(jnp.exp(sink - m))
            ov = _dot(jnp.concatenate(probs, axis=1), vcat)
            denom = ov[:, LANES:2 * LANES] + jnp.where(low, sink_terms[0], sink_terms[1])
            o = ov[:, 0:LANES] / denom
            for p in range(_PAIRS):
                c0 = (g * _PAIRS + p) * LANES
                ob_ref[r0:r0 + BLOCK, c0:c0 + LANES] = o[p * BLOCK:(p + 1) * BLOCK, :]

    kvbuf_ref[:, 0:BLOCK, :] = kvbuf_ref[:, tq:tq + BLOCK, :]
    attn = _dot(ob_ref[...], wo_ref[...]) + bo_ref[...]
    o_ref[...] = _layer_norm(ALPHA * x + attn, mg_ref[...], mb_ref[...])


def _attention(x, kv, layer, wq, bq, wo, bo, sink_cols, mg, mb):
    s, d = x.shape
    tq = ROW_TILE
    kvd = kv.shape[2]
    b_layer, trunk_layer = layer
    return pl.pallas_call(
        _attn_kernel,
        grid=(s // tq,),
        in_specs=[
            _row_spec(tq, d),
            pl.BlockSpec((4, tq, kvd), lambda i: (0, i, 0)),
            _layer_spec(b_layer, d, d), _layer_spec(b_layer, 1, d),
            _layer_spec(b_layer, d, d), _layer_spec(b_layer, 1, d),
            _const_spec(sink_cols.shape),
            _layer_spec(trunk_layer, 1, d), _layer_spec(trunk_layer, 1, d),
        ],
        out_specs=_row_spec(tq, d),
        out_shape=jax.ShapeDtypeStruct((s, d), _F32),
        scratch_shapes=[
            pltpu.VMEM((4, BLOCK + tq, kvd), _BF16),
            pltpu.VMEM((tq, d), _BF16),
            pltpu.VMEM((tq, d), _F32),
            pltpu.VMEM((2, N_KV_HEADS, _QROWS, 2 * _KEYS), _F32),
        ],
        compiler_params=_params(),
        name="swa_attention",
    )(x, kv, wq, _rows(bq), wo, _rows(bo), sink_cols, _rows(mg), _rows(mb))


def kernel(x, conv_w_pw1, conv_b_pw1, conv_w_dw, conv_b_dw, conv_ln_g, conv_ln_b, conv_w_pw2, conv_b_pw2, kv_w_k, kv_b_k, kv_w_v, kv_b_v, attn_w_q, attn_b_q, attn_sinks, attn_w_o, attn_b_o, ffn_w_gate, ffn_w_up, ffn_w_down, ln_mix_g, ln_mix_b, ln_ffn_g, ln_ffn_b):
    batch, seq, d = x.shape
    assert batch == 1, "tiles carry conv/attention history along the sequence"
    assert seq % ROW_TILE == 0 and d % (2 * HEAD_DIM) == 0
    assert conv_w_pw1.shape[0] == 1 and attn_w_q.shape[0] == 1
    assert d // HEAD_DIM == 2 * _PAIRS * N_KV_HEADS

    h = x.reshape(seq, d)
    h = _conv_mixer(
        h, (0, 0), conv_w_pw1, conv_b_pw1, conv_w_dw, conv_b_dw, conv_ln_g, conv_ln_b,
        conv_w_pw2, conv_b_pw2, ln_mix_g, ln_mix_b)

    wkv = jnp.concatenate([kv_w_k, kv_w_v], axis=1)
    bkv = jnp.concatenate([kv_b_k, kv_b_v]).reshape(1, -1)
    h, kv = _ffn(h, 0, ffn_w_gate, ffn_w_up, ffn_w_down, ln_ffn_g, ln_ffn_b, kv_proj=(wkv, bkv))

    sink_cols = attn_sinks[0].astype(_F32).reshape(N_KV_HEADS, _PAIRS, 2)
    sink_cols = jnp.repeat(sink_cols.transpose(0, 2, 1), BLOCK, axis=2)
    sink_cols = jnp.broadcast_to(sink_cols[..., None], sink_cols.shape + (LANES,))
    h = _attention(
        h, kv, (0, 1), attn_w_q, attn_b_q, attn_w_o, attn_b_o, sink_cols, ln_mix_g, ln_mix_b)

    h = _ffn(h, 1, ffn_w_gate, ffn_w_up, ffn_w_down, ln_ffn_g, ln_ffn_b)
    return h.reshape(batch, seq, d)
```
